```python
import jax
import jax.numpy as jnp
from jax import lax
import numpy as np

D_MODEL = 1024
BATCH = 1
SEQ = 16384
DEPTH = 1
DEC_BATCH = 32
DEC_SEQ = 64
PAST_LEN = 1024

CHUNK = 64
MIX_WIDTH = D_MODEL
MLSTM_WIDTH = MIX_WIDTH // 2
MLSTM_HEADS = 4
MLSTM_HEAD_DIM = MLSTM_WIDTH // MLSTM_HEADS
CONV_WIDTH = 4
POOL_WIDTH = MIX_WIDTH - MLSTM_WIDTH
POOL_WINDOWS = (2, 4, 8, 16)
POOL_GROUPS = len(POOL_WINDOWS)
POOL_GROUP_DIM = POOL_WIDTH // POOL_GROUPS
POOL_HIST = max(POOL_WINDOWS) - 1
IN_COLS = 4 * MLSTM_WIDTH + 2 * MLSTM_HEADS + POOL_WIDTH
PEER_HEADS = 8
PEER_N_KEYS = 128
PEER_N_EXPERTS = PEER_N_KEYS * PEER_N_KEYS
PEER_TOPK = 16
PEER_QUERY_DIM = 256
PEER_HALF = PEER_QUERY_DIM // 2
PEER_BLOCK = 128
RMS_EPS = 1e-6

kernel_name = 'hybrid_mlstm_pool_peer_stream_step'


def rmsnorm(x, g):
    xf = x.astype(jnp.float32)
    y = xf * lax.rsqrt(jnp.mean(xf * xf, axis=-1, keepdims=True) + RMS_EPS)
    return (y * g.astype(jnp.float32)).astype(x.dtype)


def mlstm_chunk(carry, inp):
    C, n, m = carry
    q, k, v, ig, lf = inp
    L = q.shape[2]
    b = jnp.cumsum(lf, axis=-1)
    causal = jnp.tril(jnp.ones((L, L), dtype=bool))
    D = jnp.where(causal, b[..., :, None] - b[..., None, :] + ig[..., None, :], -jnp.inf)
    inter = b + m[..., None]
    m_t = jnp.maximum(inter, jnp.max(D, axis=-1))
    w_inter = jnp.exp(inter - m_t)
    S = jnp.einsum('bhtd,bhsd->bhts', q, k) * jnp.exp(D - m_t[..., None])
    num = w_inter[..., None] * jnp.einsum('bhtd,bhde->bhte', q, C) + jnp.einsum('bhts,bhse->bhte', S, v)
    qn = w_inter * jnp.einsum('bhtd,bhd->bht', q, n) + jnp.sum(S, axis=-1)
    h = num / jnp.maximum(jnp.abs(qn), jnp.exp(-m_t))[..., None]
    m_new = m_t[..., -1]
    g_inter = jnp.exp(b[..., -1] + m - m_new)
    g_s = jnp.exp(b[..., -1:] - b + ig - m_new[..., None])
    C_new = g_inter[..., None, None] * C + jnp.einsum('bhs,bhsd,bhse->bhde', g_s, k, v)
    n_new = g_inter[..., None] * n + jnp.einsum('bhs,bhsd->bhd', g_s, k)
    return (C_new, n_new, m_new), h


def pool_mixer(xp, hist, pos0, w_pool, pool_scale):
    B, L, P = xp.shape
    full = jnp.concatenate([hist.astype(xp.dtype), xp], axis=1).astype(jnp.float32)
    cs = jnp.concatenate([jnp.zeros((B, 1, P), jnp.float32), jnp.cumsum(full, axis=1)], axis=1)
    end = cs[:, POOL_HIST + 1:]
    pos = pos0 + jnp.arange(L)
    means = []
    for g, w in enumerate(POOL_WINDOWS):
        sl = slice(g * POOL_GROUP_DIM, (g + 1) * POOL_GROUP_DIM)
        start = cs[:, POOL_HIST + 1 - w:POOL_HIST + 1 - w + L, sl]
        cnt = jnp.minimum(w, pos + 1).astype(jnp.float32)[None, :, None]
        means.append((end[..., sl] - start) / cnt)
    mean = jnp.stack(means, axis=2)
    xg = full[:, POOL_HIST:].reshape(B, L, POOL_GROUPS, POOL_GROUP_DIM)
    y = jnp.einsum('blgc,gcd->blgd', mean - xg, w_pool.astype(jnp.float32)).reshape(B, L, P)
    y = y * pool_scale.astype(jnp.float32)
    return y, full[:, -POOL_HIST:]


def token_mixers(xn, conv_h, C0, n0, m0, pool_h, pos0, w_in, b_gate, conv_w, conv_b,
                 head_norm_g, w_pool, pool_scale, w_out):
    B, L, _ = xn.shape
    W, NH, HD = MLSTM_WIDTH, MLSTM_HEADS, MLSTM_HEAD_DIM
    z = jnp.einsum('bld,de->ble', xn, w_in)
    qk_pre = z[..., :2 * W]
    v = z[..., 2 * W:3 * W]
    o_pre = z[..., 3 * W:4 * W]
    gates = z[..., 4 * W:4 * W + 2 * NH].astype(jnp.float32) + b_gate.astype(jnp.float32)
    x_pool = z[..., 4 * W + 2 * NH:]
    qk_full = jnp.concatenate([conv_h.astype(z.dtype), qk_pre], axis=1)
    qk = conv_b.astype(z.dtype) + sum(qk_full[:, j:j + L] * conv_w[j].astype(z.dtype)
                                      for j in range(CONV_WIDTH))
    qk = jax.nn.silu(qk)
    conv_new = qk_full[:, -(CONV_WIDTH - 1):]

    def heads(t):
        return t.reshape(B, L, NH, HD).transpose(0, 2, 1, 3).astype(jnp.float32)

    q = heads(qk[..., :W])
    k = heads(qk[..., W:]) * (HD ** -0.5)
    vh = heads(v)
    ig = gates[..., :NH].transpose(0, 2, 1)
    lf = jax.nn.log_sigmoid(gates[..., NH:]).transpose(0, 2, 1)
    carry = (C0.astype(jnp.float32), n0.astype(jnp.float32), m0.astype(jnp.float32))
    if L <= CHUNK:
        carry, h = mlstm_chunk(carry, (q, k, vh, ig, lf))
    else:
        nc = L // CHUNK

        def to_chunks(t):
            return jnp.moveaxis(t.reshape(t.shape[:2] + (nc, CHUNK) + t.shape[3:]), 2, 0)

        carry, hc = lax.scan(mlstm_chunk, carry, (to_chunks(q), to_chunks(k), to_chunks(vh),
                                                  to_chunks(ig), to_chunks(lf)))
        h = jnp.moveaxis(hc, 0, 2).reshape(B, NH, L, HD)
    h = h * lax.rsqrt(jnp.mean(h * h, axis=-1, keepdims=True) + RMS_EPS)
    h = h.transpose(0, 2, 1, 3).reshape(B, L, W) * head_norm_g.astype(jnp.float32)
    h = jax.nn.sigmoid(o_pre.astype(jnp.float32)) * h
    pool_out, pool_new = pool_mixer(x_pool, pool_h, pos0, w_pool, pool_scale)
    mix = jnp.concatenate([h.astype(xn.dtype), pool_out.astype(xn.dtype)], axis=-1)
    out = jnp.einsum('ble,ed->bld', mix, w_out)
    return out, carry, conv_new, pool_new


def peer_ffn(x, w_query, sub_keys, expert_u, expert_v):
    B, L, D = x.shape
    T = B * L
    pad = (-T) % PEER_BLOCK
    xb = jnp.pad(x.reshape(T, D), ((0, pad), (0, 0))).reshape(-1, PEER_BLOCK, D)
    K = PEER_TOPK

    def block(xblk):
        q = (xblk @ w_query).reshape(PEER_BLOCK, PEER_HEADS, 2, PEER_HALF).astype(jnp.float32)
        s = jnp.einsum('thpc,hpkc->thpk', q, sub_keys.astype(jnp.float32))
        s_top, i_top = lax.top_k(s, K)
        cand = (s_top[:, :, 0, :, None] + s_top[:, :, 1, None, :]).reshape(PEER_BLOCK, PEER_HEADS, K * K)
        cidx = (i_top[:, :, 0, :, None] * PEER_N_KEYS + i_top[:, :, 1, None, :]).reshape(PEER_BLOCK, PEER_HEADS, K * K)
        f_s, f_i = lax.top_k(cand, K)
        eidx = jnp.take_along_axis(cidx, f_i, axis=-1)
        g = jax.nn.softmax(f_s, axis=-1)
        u = jnp.take(expert_u, eidx, axis=0)
        a = jax.nn.gelu(jnp.einsum('thkd,td->thk', u, xblk).astype(jnp.float32), approximate=False)
        v = jnp.take(expert_v, eidx, axis=0)
        return jnp.einsum('thk,thkd->td', (g * a).astype(v.dtype), v)

    y = lax.map(block, xb).reshape(-1, D)[:T]
    return y.reshape(B, L, D).astype(x.dtype)


def trunk_layer(x, C0, n0, m0, conv_h, pool_h, pos0, norm1_g, w_in, b_gate, conv_w, conv_b,
                head_norm_g, w_pool, pool_scale, w_out, norm2_g, w_query, sub_keys, expert_u, expert_v):
    xn = rmsnorm(x, norm1_g)
    mix, (C, n, m), conv_new, pool_new = token_mixers(xn, conv_h, C0, n0, m0, pool_h, pos0, w_in, b_gate,
                                                       conv_w, conv_b, head_norm_g, w_pool, pool_scale, w_out)
    h = x + mix.astype(x.dtype)
    h = h + peer_ffn(rmsnorm(h, norm2_g), w_query, sub_keys, expert_u, expert_v)
    return h, C, n, m, conv_new, pool_new


def setup_inputs(seed: int = 0) -> dict:
    key = jax.random.key(seed)
    ks = jax.random.split(key, 24)
    f32 = jnp.float32
    nrm = lambda k, s: jax.random.normal(k, s, f32)
    W, NH, HD = MLSTM_WIDTH, MLSTM_HEADS, MLSTM_HEAD_DIM
    f_bias = jnp.linspace(3.0, 6.0, NH, dtype=f32)[None] + 0.1 * nrm(ks[9], (DEPTH, NH))
    i_bias = 0.1 * nrm(ks[10], (DEPTH, NH))
    return {
        'x_prompt': nrm(ks[0], (BATCH, SEQ, D_MODEL)),
        'x_sample': nrm(ks[1], (DEC_BATCH, DEC_SEQ, D_MODEL)),
        'state_mlstm_C': 0.1 * nrm(ks[2], (DEPTH, DEC_BATCH, NH, HD, HD)),
        'state_mlstm_n': 0.1 * nrm(ks[3], (DEPTH, DEC_BATCH, NH, HD)),
        'state_mlstm_m': 0.5 * nrm(ks[4], (DEPTH, DEC_BATCH, NH)),
        'state_conv': nrm(ks[5], (DEPTH, DEC_BATCH, CONV_WIDTH - 1, 2 * W)),
        'state_pool': nrm(ks[6], (DEPTH, DEC_BATCH, POOL_HIST, POOL_WIDTH)),
        'norm1_g': 1.0 + 0.02 * nrm(ks[7], (DEPTH, D_MODEL)),
        'w_in': nrm(ks[8], (DEPTH, D_MODEL, IN_COLS)) * D_MODEL ** -0.5,
        'b_gate': jnp.concatenate([i_bias, f_bias], axis=-1),
        'conv_w': nrm(ks[11], (DEPTH, CONV_WIDTH, 2 * W)) * CONV_WIDTH ** -0.5,
        'conv_b': 0.02 * nrm(ks[12], (DEPTH, 2 * W)),
        'head_norm_g': 1.0 + 0.02 * nrm(ks[13], (DEPTH, W)),
        'w_pool': nrm(ks[14], (DEPTH, POOL_GROUPS, POOL_GROUP_DIM, POOL_GROUP_DIM)) * POOL_GROUP_DIM ** -0.5,
        'pool_scale': 1.0 + 0.1 * nrm(ks[15], (DEPTH, POOL_WIDTH)),
        'w_out': nrm(ks[16], (DEPTH, MIX_WIDTH, D_MODEL)) * MIX_WIDTH ** -0.5,
        'norm2_g': 1.0 + 0.02 * nrm(ks[17], (DEPTH, D_MODEL)),
        'w_query': nrm(ks[18], (DEPTH, D_MODEL, PEER_HEADS * PEER_QUERY_DIM)) * D_MODEL ** -0.5,
        'sub_keys': nrm(ks[19], (DEPTH, PEER_HEADS, 2, PEER_N_KEYS, PEER_HALF)) * PEER_HALF ** -0.5,
        'expert_u': nrm(ks[20], (DEPTH, PEER_N_EXPERTS, D_MODEL)) * D_MODEL ** -0.5,
        'expert_v': 0.5 * nrm(ks[21], (DEPTH, PEER_N_EXPERTS, D_MODEL)),
        'final_norm_g': 1.0 + 0.02 * nrm(ks[22], (D_MODEL,)),
    }


def reference(x_prompt, x_sample, state_mlstm_C, state_mlstm_n, state_mlstm_m, state_conv, state_pool,
              norm1_g, w_in, b_gate, conv_w, conv_b, head_norm_g, w_pool, pool_scale, w_out,
              norm2_g, w_query, sub_keys, expert_u, expert_v, final_norm_g):
    yp, ys = x_prompt, x_sample
    Bp = x_prompt.shape[0]
    dt = x_prompt.dtype
    new_p, new_s = [], []
    for l in range(DEPTH):
        w = (norm1_g[l], w_in[l], b_gate[l], conv_w[l], conv_b[l], head_norm_g[l], w_pool[l],
             pool_scale[l], w_out[l], norm2_g[l], w_query[l], sub_keys[l], expert_u[l], expert_v[l])
        zC = jnp.zeros((Bp, MLSTM_HEADS, MLSTM_HEAD_DIM, MLSTM_HEAD_DIM), jnp.float32)
        zn = jnp.zeros((Bp, MLSTM_HEADS, MLSTM_HEAD_DIM), jnp.float32)
        zm = jnp.zeros((Bp, MLSTM_HEADS), jnp.float32)
        zconv = jnp.zeros((Bp, CONV_WIDTH - 1, 2 * MLSTM_WIDTH), dt)
        zpool = jnp.zeros((Bp, POOL_HIST, POOL_WIDTH), dt)
        yp, *sp = trunk_layer(yp, zC, zn, zm, zconv, zpool, 0, *w)
        ys, *ss = trunk_layer(ys, state_mlstm_C[l], state_mlstm_n[l], state_mlstm_m[l], state_conv[l],
                              state_pool[l], PAST_LEN, *w)
        new_p.append(sp)
        new_s.append(ss)

    def stk(lst, i):
        return jnp.stack([e[i] for e in lst])

    return (rmsnorm(yp, final_norm_g), rmsnorm(ys, final_norm_g),
            stk(new_p, 0), stk(new_p, 1), stk(new_p, 2), stk(new_p, 3), stk(new_p, 4),
            stk(new_s, 0), stk(new_s, 1), stk(new_s, 2), stk(new_s, 3), stk(new_s, 4))
```

```python
import functools

import jax
import jax.numpy as jnp
from jax import lax
from jax.experimental import pallas as pl
from jax.experimental.pallas import tpu as pltpu

F32 = jnp.float32
BF16 = jnp.bfloat16

LANES = 128
SUBLANES = 8
VMEM_LIMIT_BYTES = 56 * 1024 * 1024

D_MODEL = 1024
CHUNK = 64
NH = 4
HD = 128
MW = NH * HD
CONV_WIDTH = 4
POOL_WINDOWS = (2, 4, 8, 16)
PG = len(POOL_WINDOWS)
PGD = 128
PW = PG * PGD
CONV_PAD = SUBLANES
POOL_PAD = 2 * SUBLANES
PEER_HEADS = 8
PEER_KEYS = 128
PEER_TOPK = 16
PEER_HALF = 128
N_EXPERTS = PEER_KEYS * PEER_KEYS
RMS_EPS = 1e-6
NEG_INF = float("-inf")

MIX_BLOCK = 256
ROUTE_BLOCK = 256
DENSE_BLOCK = 512
EXPERT_CHUNK = 1024
TOKEN_LANES = 128


def _dot(a, b):
    return jnp.dot(a, b, preferred_element_type=F32)


def _dot_nt(a, b, precision=None):
    return lax.dot_general(a, b, (((1,), (1,)), ((), ())), preferred_element_type=F32, precision=precision)


def _dot_tn(a, b):
    return lax.dot_general(a, b, (((0,), (0,)), ((), ())), preferred_element_type=F32)


def _rmsnorm(x, g):
    return x * lax.rsqrt(jnp.mean(x * x, axis=-1, keepdims=True) + RMS_EPS) * g


def _split_bf16(a):
    hi = a.astype(BF16)
    lo = (a - hi.astype(F32)).astype(BF16)
    return hi, lo


def _mixer_body(chained, pos0, *refs):
    if chained:
        (x_ref, g1_ref, wmain_ref, wgate_ref, bgate_ref, convw_ref, convb_ref, hng_ref, wpool_ref,
         pscale_ref, wout_ref, g2_ref,
         hres_ref, hn_ref, c_out, n_out, m_out, conv_out, pool_out,
         z_ref, gate_ref, mix_ref, qkbuf, poolbuf, c_st, n_st, m_st) = refs
    else:
        (x_ref, c_in, n_in, m_in, conv_in, pool_in,
         g1_ref, wmain_ref, wgate_ref, bgate_ref, convw_ref, convb_ref, hng_ref, wpool_ref,
         pscale_ref, wout_ref, g2_ref,
         hres_ref, hn_ref, c_out, n_out, m_out, conv_out, pool_out,
         z_ref, gate_ref, mix_ref, qkbuf, poolbuf, c_st, n_st, m_st) = refs
    step = pl.program_id(0)
    nb = x_ref.shape[0] // CHUNK

    x = x_ref[...]
    xb = _rmsnorm(x, g1_ref[...]).astype(BF16)
    z_ref[...] = _dot(xb, wmain_ref[...])
    gates = _dot(xb, wgate_ref[...]) + bgate_ref[...]
    log_f = jnp.minimum(gates, 0.0) - jnp.log1p(jnp.exp(-jnp.abs(gates)))
    lane = lax.broadcasted_iota(jnp.int32, gates.shape, 1)
    gate_ref[...] = jnp.where(lane < NH, gates, log_f)

    if chained:
        @pl.when(step == 0)
        def _():
            qkbuf[0:CONV_PAD, :] = jnp.zeros((CONV_PAD, 2 * MW), F32)
            poolbuf[0:POOL_PAD, :] = jnp.zeros((POOL_PAD, PW), F32)
            c_st[...] = jnp.zeros(c_st.shape, F32)
            n_st[...] = jnp.zeros(n_st.shape, F32)
            m_st[...] = jnp.zeros(m_st.shape, F32)

    row = lax.broadcasted_iota(jnp.int32, (CHUNK, CHUNK), 0)
    col = lax.broadcasted_iota(jnp.int32, (CHUNK, CHUNK), 1)
    causal = col <= row
    tril = causal.astype(F32)
    pick = (lax.broadcasted_iota(jnp.int32, (SUBLANES, LANES), 0)
            == lax.broadcasted_iota(jnp.int32, (SUBLANES, LANES), 1)).astype(F32)
    lane_c = lax.broadcasted_iota(jnp.int32, (CHUNK, LANES), 1)
    tok = lax.broadcasted_iota(jnp.int32, (CHUNK, 1), 0)

    def chunk(c, carry):
        r0 = pl.multiple_of(c * CHUNK, CHUNK)
        rows = pl.ds(r0, CHUNK)
        if not chained:
            qkbuf[0:CONV_PAD, :] = conv_in[c]
            poolbuf[0:POOL_PAD, :] = pool_in[c]
            c_st[...] = c_in[c]
            n_st[...] = n_in[c]
            m_st[...] = m_in[c]

        qkbuf[CONV_PAD:CONV_PAD + CHUNK, :] = z_ref[rows, 0:2 * MW]
        acc = jnp.broadcast_to(convb_ref[...], (CHUNK, 2 * MW))
        for j in range(CONV_WIDTH):
            off = CONV_PAD - (CONV_WIDTH - 1) + j
            acc = acc + qkbuf[off:off + CHUNK, :] * convw_ref[j:j + 1, :]
        qk = acc * jax.nn.sigmoid(acc)
        qkbuf[0:CONV_PAD, :] = qkbuf[CHUNK:CHUNK + CONV_PAD, :]

        g_c = gate_ref[rows, :]
        cum = jnp.dot(tril, g_c, preferred_element_type=F32, precision=lax.Precision.HIGHEST)
        gb = jnp.where(lane_c < NH, g_c, cum)
        gb_rows = _dot_nt(pick, gb, precision=lax.Precision.HIGHEST)

        for h in range(NH):
            q = qk[:, h * HD:(h + 1) * HD]
            k = qk[:, MW + h * HD:MW + (h + 1) * HD] * (HD ** -0.5)
            v = z_ref[rows, 2 * MW + h * HD:2 * MW + (h + 1) * HD]
            o_pre = z_ref[rows, 3 * MW + h * HD:3 * MW + (h + 1) * HD]
            qb, kb, vb = q.astype(BF16), k.astype(BF16), v.astype(BF16)
            ig_col = gb[:, h:h + 1]
            b_col = gb[:, NH + h:NH + h + 1]
            ig_row = gb_rows[h:h + 1, :]
            b_row = gb_rows[NH + h:NH + h + 1, :]
            m_prev = m_st[h:h + 1, 0:1]
            c_prev = c_st[h]
            n_prev = n_st[h:h + 1, :]

            d = jnp.where(causal, b_col - b_row + ig_row, NEG_INF)
            inter = b_col + m_prev
            m_t = jnp.maximum(inter, jnp.max(d, axis=1, keepdims=True))
            w_inter = jnp.exp(inter - m_t)
            s = _dot_nt(qb, kb) * jnp.exp(d - m_t)
            num = w_inter * _dot(qb, c_prev.astype(BF16)) + _dot(s.astype(BF16), vb)
            qn = (w_inter * jnp.sum(q * n_prev, axis=1, keepdims=True)
                  + jnp.sum(s, axis=1, keepdims=True))
            hh = num / jnp.maximum(jnp.abs(qn), jnp.exp(-m_t))

            m_new = m_t[CHUNK - 1:CHUNK, :]
            b_last = b_col[CHUNK - 1:CHUNK, :]
            g_inter = jnp.exp(b_last + m_prev - m_new)
            g_s = jnp.exp(b_last - b_col + ig_col - m_new)
            c_st[h] = g_inter * c_prev + _dot_tn(kb, (g_s * v).astype(BF16))
            n_st[h:h + 1, :] = g_inter * n_prev + jnp.sum(g_s * k, axis=0, keepdims=True)
            m_st[h:h + 1, :] = jnp.broadcast_to(m_new, (1, LANES))

            hh = hh * lax.rsqrt(jnp.mean(hh * hh, axis=1, keepdims=True) + RMS_EPS)
            hh = hh * hng_ref[:, h * HD:(h + 1) * HD]
            mix_ref[rows, h * HD:(h + 1) * HD] = jax.nn.sigmoid(o_pre) * hh

        poolbuf[POOL_PAD:POOL_PAD + CHUNK, :] = z_ref[rows, 4 * MW:4 * MW + PW]
        if chained:
            pos = pos0 + step * x_ref.shape[0] + r0 + tok
        else:
            pos = pos0 + tok
        for g, w in enumerate(POOL_WINDOWS):
            cols = slice(g * PGD, (g + 1) * PGD)
            xg = poolbuf[POOL_PAD:POOL_PAD + CHUNK, cols]
            tot = xg
            for dlt in range(1, w):
                tot = tot + poolbuf[POOL_PAD - dlt:POOL_PAD - dlt + CHUNK, cols]
            cnt = jnp.minimum(w, pos + 1).astype(F32)
            y = _dot((tot / cnt - xg).astype(BF16), wpool_ref[g]) * pscale_ref[:, cols]
            mix_ref[rows, MW + g * PGD:MW + (g + 1) * PGD] = y
        poolbuf[0:POOL_PAD, :] = poolbuf[CHUNK:CHUNK + POOL_PAD, :]

        if not chained:
            c_out[c] = c_st[...]
            n_out[c] = n_st[...]
            m_out[c] = m_st[...]
            conv_out[c] = qkbuf[0:CONV_PAD, :]
            pool_out[c] = poolbuf[0:POOL_PAD, :]
        return carry

    lax.fori_loop(0, nb, chunk, 0)

    if chained:
        c_out[...] = c_st[...]
        n_out[...] = n_st[...]
        m_out[...] = m_st[...]
        conv_out[...] = qkbuf[0:CONV_PAD, :]
        pool_out[...] = poolbuf[0:POOL_PAD, :]

    hres = x + _dot(mix_ref[...].astype(BF16), wout_ref[...])
    hres_ref[...] = hres
    hn_ref[...] = _rmsnorm(hres, g2_ref[...])


def _const_spec(shape):
    nd = len(shape)
    return pl.BlockSpec(shape, lambda i: (0,) * nd)


def _mixer(x, state, weights, *, pos0):
    t = x.shape[0]
    chained = state is None
    blk = MIX_BLOCK
    nb = blk // CHUNK
    ns = 1 if chained else t // CHUNK
    tok_spec = pl.BlockSpec((blk, D_MODEL), lambda i: (i, 0))
    w_specs = [_const_spec(w.shape) for w in weights]
    if chained:
        in_specs = [tok_spec] + w_specs
        args = (x,) + tuple(weights)
        st_specs = [_const_spec((NH, HD, HD)), _const_spec((SUBLANES, LANES)), _const_spec((SUBLANES, LANES)),
                    _const_spec((CONV_PAD, 2 * MW)), _const_spec((POOL_PAD, PW))]
        st_shapes = [(NH, HD, HD), (SUBLANES, LANES), (SUBLANES, LANES), (CONV_PAD, 2 * MW), (POOL_PAD, PW)]
    else:
        def sspec(*tail):
            nd = len(tail)
            return pl.BlockSpec((nb,) + tail, lambda i: (i,) + (0,) * nd)
        st_specs = [sspec(NH, HD, HD), sspec(SUBLANES, LANES), sspec(SUBLANES, LANES),
                    sspec(CONV_PAD, 2 * MW), sspec(POOL_PAD, PW)]
        st_shapes = [(ns, NH, HD, HD), (ns, SUBLANES, LANES), (ns, SUBLANES, LANES),
                     (ns, CONV_PAD, 2 * MW), (ns, POOL_PAD, PW)]
        in_specs = [tok_spec] + st_specs + w_specs
        args = (x,) + tuple(state) + tuple(weights)
    out_shape = ([jax.ShapeDtypeStruct((t, D_MODEL), F32)] * 2
                 + [jax.ShapeDtypeStruct(s, F32) for s in st_shapes])
    scratch = [
        pltpu.VMEM((blk, 4 * MW + PW), F32),
        pltpu.VMEM((blk, LANES), F32),
        pltpu.VMEM((blk, 2 * MW), F32),
        pltpu.VMEM((CONV_PAD + CHUNK, 2 * MW), F32),
        pltpu.VMEM((POOL_PAD + CHUNK, PW), F32),
        pltpu.VMEM((NH, HD, HD), F32),
        pltpu.VMEM((SUBLANES, LANES), F32),
        pltpu.VMEM((SUBLANES, LANES), F32),
    ]
    return pl.pallas_call(
        functools.partial(_mixer_body, chained, pos0),
        grid=(t // blk,),
        in_specs=in_specs,
        out_specs=[tok_spec, tok_spec] + st_specs,
        out_shape=out_shape,
        scratch_shapes=scratch,
        compiler_params=pltpu.CompilerParams(dimension_semantics=("arbitrary",),
                                             vmem_limit_bytes=VMEM_LIMIT_BYTES),
        name="mixer_chained" if chained else "mixer_streams",
    )(*args)


_CAND_GROUPS = (
    (("range", 0, 16), ("bcast", 0), None),
    (("range", 0, 8), ("bcast", 1), None),
    (("bcast", 0), ("range", 8, 16), None),
    (("bcast", 0), ("range", 0, 8), (2, 7)),
    (("bcast", 1), ("range", 0, 8), (2, 7)),
    (("bcast", 2), ("range", 0, 8), (2, 4)),
    (("bcast", 3), ("range", 0, 8), (2, 3)),
    (("bcast", 4), ("range", 0, 8), (2, 2)),
)


def _top16_by_rounds(s, key_id):
    vals, ids = [], []
    for _ in range(PEER_TOPK):
        mx = jnp.max(s, axis=0, keepdims=True)
        pid = jnp.min(jnp.where(s == mx, key_id, 1e9), axis=0, keepdims=True)
        s = jnp.where(key_id == pid, NEG_INF, s)
        vals.append(mx)
        ids.append(pid)
    return vals, ids


def _route_body(hn_ref, wqh_ref, wql_ref, keys_ref, r2_ref, e2_ref, c1_ref, e1_ref, q_ref, s_ref):
    ntg = hn_ref.shape[0] // TOKEN_LANES
    hi, lo = _split_bf16(hn_ref[...])
    q_ref[...] = _dot(hi, wqh_ref[...]) + _dot(hi, wql_ref[...]) + _dot(lo, wqh_ref[...])
    for hp in range(2 * PEER_HEADS):
        k_hi, k_lo = _split_bf16(keys_ref[hp])
        q_hi, q_lo = _split_bf16(q_ref[:, hp * PEER_HALF:(hp + 1) * PEER_HALF])
        s_t = _dot_nt(k_hi, q_hi) + _dot_nt(k_hi, q_lo) + _dot_nt(k_lo, q_hi)
        for tg in range(ntg):
            s_ref[hp, tg] = s_t[:, tg * TOKEN_LANES:(tg + 1) * TOKEN_LANES]

    key_id = lax.broadcasted_iota(jnp.int32, (PEER_KEYS, TOKEN_LANES), 0).astype(F32)
    rank16 = lax.broadcasted_iota(jnp.int32, (PEER_TOPK, TOKEN_LANES), 0).astype(F32)
    sub = lax.broadcasted_iota(jnp.int32, (SUBLANES, TOKEN_LANES), 0)
    pos_rows, valid_rows = [], []
    for ga, gb, vb in _CAND_GROUPS:
        n = ga[2] - ga[1] if ga[0] == "range" else gb[2] - gb[1]
        r = lax.broadcasted_iota(jnp.int32, (n, TOKEN_LANES), 0)
        a_idx = r + ga[1] if ga[0] == "range" else jnp.full_like(r, ga[1])
        b_idx = r + gb[1] if gb[0] == "range" else jnp.full_like(r, gb[1])
        pos_rows.append((a_idx * PEER_TOPK + b_idx).astype(F32))
        valid_rows.append(jnp.ones(r.shape, jnp.bool_) if vb is None else (b_idx >= vb[0]) & (b_idx <= vb[1]))
    cand_pos = jnp.concatenate(pos_rows, axis=0)
    cand_valid = jnp.concatenate(valid_rows, axis=0)
    del sub

    def ranks_of(ids):
        rank = jnp.full((PEER_KEYS, TOKEN_LANES), float(PEER_TOPK), F32)
        for a, pid in enumerate(ids):
            rank = jnp.where(key_id == pid, float(a), rank)
        return rank

    def one(idx, carry):
        h = idx // ntg
        tg = idx % ntg
        s1 = s_ref[2 * h, tg]
        s2 = s_ref[2 * h + 1, tg]
        v1, id1 = _top16_by_rounds(s1, key_id)
        v2, id2 = _top16_by_rounds(s2, key_id)
        r1 = ranks_of(id1)
        r2 = ranks_of(id2)
        v1a = jnp.concatenate(v1, axis=0)
        v2a = jnp.concatenate(v2, axis=0)

        def side(arr, spec):
            if spec[0] == "range":
                return arr[spec[1]:spec[2], :]
            return arr[spec[1]:spec[1] + 1, :]

        cand = jnp.concatenate([side(v1a, ga) + side(v2a, gb) for ga, gb, _ in _CAND_GROUPS], axis=0)
        cand = jnp.where(cand_valid, cand, NEG_INF)
        top = v1[0] + v2[0]
        picked, pos = _top16_by_rounds(cand, cand_pos)
        z = jnp.zeros_like(top)
        cnt = jnp.zeros((PEER_TOPK, TOKEN_LANES), F32)
        for val, p in zip(picked, pos):
            z = z + jnp.exp(val - top)
            cnt = cnt + jnp.where(rank16 == jnp.floor(p * (1.0 / PEER_TOPK)), 1.0, 0.0)
        c1 = jnp.zeros((PEER_KEYS, TOKEN_LANES), F32)
        for a in range(PEER_TOPK):
            c1 = jnp.where(r1 == float(a), cnt[a:a + 1, :], c1)
        r2_ref[h, tg] = r2.astype(BF16)
        e2_ref[h, tg] = jnp.exp(s2 - v2[0]).astype(BF16)
        c1_ref[h, tg] = c1
        e1_ref[h, tg] = jnp.exp(s1 - v1[0]) / z
        return carry

    lax.fori_loop(0, PEER_HEADS * ntg, one, 0)


def _peer_route(hn, wq_hi, wq_lo, keys):
    t = hn.shape[0]
    blk = ROUTE_BLOCK
    ntg = blk // TOKEN_LANES
    tab = (PEER_HEADS, t // TOKEN_LANES, PEER_KEYS, TOKEN_LANES)
    tab_spec = pl.BlockSpec((PEER_HEADS, ntg, PEER_KEYS, TOKEN_LANES), lambda i: (0, i, 0, 0))
    return pl.pallas_call(
        _route_body,
        grid=(t // blk,),
        in_specs=[pl.BlockSpec((blk, D_MODEL), lambda i: (i, 0)), _const_spec(wq_hi.shape),
                  _const_spec(wq_lo.shape), _const_spec(keys.shape)],
        out_specs=[tab_spec] * 4,
        out_shape=[jax.ShapeDtypeStruct(tab, BF16), jax.ShapeDtypeStruct(tab, BF16),
                   jax.ShapeDtypeStruct(tab, F32), jax.ShapeDtypeStruct(tab, F32)],
        scratch_shapes=[pltpu.VMEM((blk, 2 * PEER_HEADS * PEER_HALF), F32),
                        pltpu.VMEM((2 * PEER_HEADS, ntg, PEER_KEYS, TOKEN_LANES), F32)],
        compiler_params=pltpu.CompilerParams(dimension_semantics=("arbitrary",),
                                             vmem_limit_bytes=VMEM_LIMIT_BYTES),
        name="peer_route",
    )(hn, wq_hi, wq_lo, keys)


def _dense_body(final, hn_ref, hres_ref, r2_ref, e2_ref, c1_ref, e1_ref, u_ref, vt_ref, fg_ref,
                y_ref, acc_ref, p_ref):
    c = pl.program_id(1)
    ntg = hn_ref.shape[0] // TOKEN_LANES
    rows_per_step = u_ref.shape[0] // PEER_KEYS

    @pl.when(c == 0)
    def _():
        acc_ref[...] = jnp.zeros(acc_ref.shape, F32)

    xb = hn_ref[...].astype(BF16)

    def one_row(ii, carry):
        e0 = pl.multiple_of(ii * PEER_KEYS, PEER_KEYS)
        i = c * rows_per_step + ii
        a_t = _dot_nt(u_ref[pl.ds(e0, PEER_KEYS), :], xb)
        for tg in range(ntg):
            a = a_t[:, tg * TOKEN_LANES:(tg + 1) * TOKEN_LANES]
            act = a * (0.5 + 0.5 * lax.erf(a * (0.5 ** 0.5)))
            w = jnp.zeros((PEER_KEYS, TOKEN_LANES), BF16)
            for h in range(PEER_HEADS):
                cnt_i = c1_ref[h, tg, pl.ds(i, 1), :].astype(BF16)
                e1_i = e1_ref[h, tg, pl.ds(i, 1), :].astype(BF16)
                w = w + jnp.where(r2_ref[h, tg] < cnt_i, e2_ref[h, tg], jnp.zeros((), BF16)) * e1_i
            p_ref[pl.ds(e0, PEER_KEYS), tg * TOKEN_LANES:(tg + 1) * TOKEN_LANES] = w * act.astype(BF16)
        return carry

    lax.fori_loop(0, rows_per_step, one_row, 0)
    acc_ref[...] += _dot(vt_ref[...], p_ref[...])

    @pl.when(c == pl.num_programs(1) - 1)
    def _():
        y = hres_ref[...] + acc_ref[...].T
        if final:
            y = _rmsnorm(y, fg_ref[...])
        y_ref[...] = y


def _peer_dense(hn, hres, tables, u_bf, vt_bf, fg, *, final):
    t = hn.shape[0]
    blk = DENSE_BLOCK
    ntg = blk // TOKEN_LANES
    tok_spec = pl.BlockSpec((blk, D_MODEL), lambda i, c: (i, 0))
    tab_spec = pl.BlockSpec((PEER_HEADS, ntg, PEER_KEYS, TOKEN_LANES), lambda i, c: (0, i, 0, 0))
    return pl.pallas_call(
        functools.partial(_dense_body, final),
        grid=(t // blk, N_EXPERTS // EXPERT_CHUNK),
        in_specs=[tok_spec, tok_spec, tab_spec, tab_spec, tab_spec, tab_spec,
                  pl.BlockSpec((EXPERT_CHUNK, D_MODEL), lambda i, c: (c, 0)),
                  pl.BlockSpec((D_MODEL, EXPERT_CHUNK), lambda i, c: (0, c)),
                  pl.BlockSpec((1, D_MODEL), lambda i, c: (0, 0))],
        out_specs=tok_spec,
        out_shape=jax.ShapeDtypeStruct((t, D_MODEL), F32),
        scratch_shapes=[pltpu.VMEM((D_MODEL, blk), F32), pltpu.VMEM((EXPERT_CHUNK, blk), BF16)],
        compiler_params=pltpu.CompilerParams(dimension_semantics=("arbitrary", "arbitrary"),
                                             vmem_limit_bytes=VMEM_LIMIT_BYTES),
        name="peer_dense",
    )(hn, hres, *tables, u_bf, vt_bf, fg)


PAST_LEN = 1024


def _row(a):
    return a.reshape(1, -1).astype(F32)


def _layer(x, state, pos0, mixer_w, peer_w, fg, final):
    hres, hn, *new_state = _mixer(x, state, mixer_w, pos0=pos0)
    wq_hi, wq_lo, keys, u_bf, vt_bf = peer_w
    tables = _peer_route(hn, wq_hi, wq_lo, keys)
    y = _peer_dense(hn, hres, tables, u_bf, vt_bf, fg, final=final)
    return y, new_state


def kernel(x_prompt, x_sample, state_mlstm_C, state_mlstm_n, state_mlstm_m, state_conv, state_pool, norm1_g, w_in, b_gate, conv_w, conv_b, head_norm_g, w_pool, pool_scale, w_out, norm2_g, w_query, sub_keys, expert_u, expert_v, final_norm_g):
    depth = w_in.shape[0]
    bp, seq, d = x_prompt.shape
    bs, dseq, _ = x_sample.shape
    assert bp == 1 and dseq == CHUNK and d == D_MODEL
    yp = x_prompt.reshape(seq, d)
    ys = x_sample.reshape(bs * dseq, d)
    fg = _row(final_norm_g)
    new_p, new_s = [], []
    for l in range(depth):
        gate0 = 4 * MW
        gate1 = gate0 + 2 * NH
        w_main = jnp.concatenate([w_in[l][:, :gate0], w_in[l][:, gate1:]], axis=1).astype(BF16)
        w_gate = jnp.pad(w_in[l][:, gate0:gate1], ((0, 0), (0, LANES - 2 * NH))).astype(BF16)
        bg = jnp.pad(b_gate[l].astype(F32), (0, LANES - 2 * NH)).reshape(1, LANES)
        mixer_w = (_row(norm1_g[l]), w_main, w_gate, bg, conv_w[l].astype(F32), _row(conv_b[l]),
                   _row(head_norm_g[l]), w_pool[l].astype(BF16), _row(pool_scale[l]),
                   w_out[l].astype(BF16), _row(norm2_g[l]))
        wq_hi, wq_lo = _split_bf16(w_query[l].astype(F32))
        keys = sub_keys[l].astype(F32).reshape(2 * PEER_HEADS, PEER_KEYS, PEER_HALF)
        peer_w = (wq_hi, wq_lo, keys, expert_u[l].astype(BF16), expert_v[l].T.astype(BF16))
        state = (
            state_mlstm_C[l].astype(F32),
            jnp.pad(state_mlstm_n[l].astype(F32), ((0, 0), (0, SUBLANES - NH), (0, 0))),
            jnp.broadcast_to(jnp.pad(state_mlstm_m[l].astype(F32), ((0, 0), (0, SUBLANES - NH)))[:, :, None],
                             (bs, SUBLANES, LANES)),
            jnp.pad(state_conv[l].astype(F32), ((0, 0), (CONV_PAD - (CONV_WIDTH - 1), 0), (0, 0))),
            jnp.pad(state_pool[l].astype(F32), ((0, 0), (1, 0), (0, 0))),
        )
        final = l == depth - 1
        yp, sp = _layer(yp, None, 0, mixer_w, peer_w, fg, final)
        ys, ss = _layer(ys, state, PAST_LEN, mixer_w, peer_w, fg, final)
        c, n, m, cv, po = sp
        new_p.append((c[None], n[None, :NH], m[None, :NH, 0], cv[None, CONV_PAD - (CONV_WIDTH - 1):], po[None, 1:]))
        c, n, m, cv, po = ss
        new_s.append((c, n[:, :NH], m[:, :NH, 0], cv[:, CONV_PAD - (CONV_WIDTH - 1):], po[:, 1:]))

    def stk(lst, i):
        return jnp.stack([e[i] for e in lst])

    return (yp.reshape(bp, seq, d), ys.reshape(bs, dseq, d),
            stk(new_p, 0), stk(new_p, 1), stk(new_p, 2), stk(new_p, 3), stk(new_p, 4),
            stk(new_s, 0), stk(new_s, 1), stk(new_s, 2), stk(new_s, 3), stk(new_s, 4))
```

```python
import functools

import jax
import jax.numpy as jnp
from jax import lax
from jax.experimental import pallas as pl
from jax.experimental.pallas import tpu as pltpu

F32 = jnp.float32
BF16 = jnp.bfloat16
U32 = jnp.uint32

LANES = 128
SUBLANES = 8
VMEM_LIMIT_BYTES = 56 * 1024 * 1024

D_MODEL = 1024
CHUNK = 64
NH = 4
HD = 128
MW = NH * HD
CONV_WIDTH = 4
POOL_WINDOWS = (2, 4, 8, 16)
PG = len(POOL_WINDOWS)
PGD = 128
PW = PG * PGD
CONV_PAD = SUBLANES
POOL_PAD = 2 * SUBLANES
PEER_HEADS = 8
PEER_KEYS = 128
PEER_TOPK = 16
PEER_HALF = 128
N_EXPERTS = PEER_KEYS * PEER_KEYS
RMS_EPS = 1e-6
NEG_INF = float("-inf")

MIX_BLOCK = 256
ROUTE_BLOCK = 256
DENSE_BLOCK = 512
EXPERT_CHUNK = 1024
TOKEN_LANES = 128
DENSE_M_PIECE = 512
DENSE_N_PIECE = 256
DENSE_TILE_ROWS = 128


def _dot(a, b):
    return jnp.dot(a, b, preferred_element_type=F32)


def _dot_nt(a, b, precision=None):
    return lax.dot_general(a, b, (((1,), (1,)), ((), ())), preferred_element_type=F32, precision=precision)


def _dot_tn(a, b):
    return lax.dot_general(a, b, (((0,), (0,)), ((), ())), preferred_element_type=F32)


def _rmsnorm(x, g):
    return x * lax.rsqrt(jnp.mean(x * x, axis=-1, keepdims=True) + RMS_EPS) * g


def _split_bf16(a):
    hi = a.astype(BF16)
    lo = (a - hi.astype(F32)).astype(BF16)
    return hi, lo


def _pack_rows(a):
    return pltpu.bitcast(a, U32)


def _unpack_rows(a):
    return pltpu.bitcast(a, BF16)


def _mixer_body(chained, pos0, *refs):
    if chained:
        (x_ref, g1_ref, wmain_ref, wgate_ref, bgate_ref, convw_ref, convb_ref, hng_ref, wpool_ref,
         pscale_ref, wout_ref, g2_ref,
         hres_ref, hn_ref, hnb_ref, c_out, n_out, m_out, conv_out, pool_out,
         z_ref, gate_ref, mix_ref, qkbuf, poolbuf, c_st, n_st, m_st) = refs
    else:
        (x_ref, c_in, n_in, m_in, conv_in, pool_in,
         g1_ref, wmain_ref, wgate_ref, bgate_ref, convw_ref, convb_ref, hng_ref, wpool_ref,
         pscale_ref, wout_ref, g2_ref,
         hres_ref, hn_ref, hnb_ref, c_out, n_out, m_out, conv_out, pool_out,
         z_ref, gate_ref, mix_ref, qkbuf, poolbuf, c_st, n_st, m_st) = refs
    step = pl.program_id(0)
    nb = x_ref.shape[0] // CHUNK

    x = x_ref[...]
    xb = _rmsnorm(x, g1_ref[...]).astype(BF16)
    z_ref[...] = _dot(xb, wmain_ref[...])
    gates = _dot(xb, wgate_ref[...]) + bgate_ref[...]
    log_f = jnp.minimum(gates, 0.0) - jnp.log1p(jnp.exp(-jnp.abs(gates)))
    lane = lax.broadcasted_iota(jnp.int32, gates.shape, 1)
    gate_ref[...] = jnp.where(lane < NH, gates, log_f)

    if chained:
        @pl.when(step == 0)
        def _():
            qkbuf[0:CONV_PAD, :] = jnp.zeros((CONV_PAD, 2 * MW), F32)
            poolbuf[0:POOL_PAD, :] = jnp.zeros((POOL_PAD, PW), F32)
            c_st[...] = jnp.zeros(c_st.shape, F32)
            n_st[...] = jnp.zeros(n_st.shape, F32)
            m_st[...] = jnp.zeros(m_st.shape, F32)

    row = lax.broadcasted_iota(jnp.int32, (CHUNK, CHUNK), 0)
    col = lax.broadcasted_iota(jnp.int32, (CHUNK, CHUNK), 1)
    causal = col <= row
    tril = causal.astype(F32)
    pick = (lax.broadcasted_iota(jnp.int32, (SUBLANES, LANES), 0)
            == lax.broadcasted_iota(jnp.int32, (SUBLANES, LANES), 1)).astype(F32)
    lane_c = lax.broadcasted_iota(jnp.int32, (CHUNK, LANES), 1)
    tok = lax.broadcasted_iota(jnp.int32, (CHUNK, 1), 0)

    def chunk(c, carry):
        r0 = pl.multiple_of(c * CHUNK, CHUNK)
        rows = pl.ds(r0, CHUNK)
        if not chained:
            qkbuf[0:CONV_PAD, :] = conv_in[c]
            poolbuf[0:POOL_PAD, :] = pool_in[c]
            c_st[...] = c_in[c]
            n_st[...] = n_in[c]
            m_st[...] = m_in[c]

        qkbuf[CONV_PAD:CONV_PAD + CHUNK, :] = z_ref[rows, 0:2 * MW]
        acc = jnp.broadcast_to(convb_ref[...], (CHUNK, 2 * MW))
        for j in range(CONV_WIDTH):
            off = CONV_PAD - (CONV_WIDTH - 1) + j
            acc = acc + qkbuf[off:off + CHUNK, :] * convw_ref[j:j + 1, :]
        qk = acc * jax.nn.sigmoid(acc)
        qkbuf[0:CONV_PAD, :] = qkbuf[CHUNK:CHUNK + CONV_PAD, :]

        g_c = gate_ref[rows, :]
        cum = jnp.dot(tril, g_c, preferred_element_type=F32, precision=lax.Precision.HIGHEST)
        gb = jnp.where(lane_c < NH, g_c, cum)
        gb_rows = _dot_nt(pick, gb, precision=lax.Precision.HIGHEST)

        for h in range(NH):
            q = qk[:, h * HD:(h + 1) * HD]
            k = qk[:, MW + h * HD:MW + (h + 1) * HD] * (HD ** -0.5)
            v = z_ref[rows, 2 * MW + h * HD:2 * MW + (h + 1) * HD]
            o_pre = z_ref[rows, 3 * MW + h * HD:3 * MW + (h + 1) * HD]
            qb, kb, vb = q.astype(BF16), k.astype(BF16), v.astype(BF16)
            ig_col = gb[:, h:h + 1]
            b_col = gb[:, NH + h:NH + h + 1]
            ig_row = gb_rows[h:h + 1, :]
            b_row = gb_rows[NH + h:NH + h + 1, :]
            m_prev = m_st[h:h + 1, 0:1]
            c_prev = c_st[h]
            n_prev = n_st[h:h + 1, :]

            d = jnp.where(causal, b_col - b_row + ig_row, NEG_INF)
            inter = b_col + m_prev
            m_t = jnp.maximum(inter, jnp.max(d, axis=1, keepdims=True))
            w_inter = jnp.exp(inter - m_t)
            s = _dot_nt(qb, kb) * jnp.exp(d - m_t)
            num = w_inter * _dot(qb, c_prev.astype(BF16)) + _dot(s.astype(BF16), vb)
            qn = (w_inter * jnp.sum(q * n_prev, axis=1, keepdims=True)
                  + jnp.sum(s, axis=1, keepdims=True))
            hh = num / jnp.maximum(jnp.abs(qn), jnp.exp(-m_t))

            m_new = m_t[CHUNK - 1:CHUNK, :]
            b_last = b_col[CHUNK - 1:CHUNK, :]
            g_inter = jnp.exp(b_last + m_prev - m_new)
            g_s = jnp.exp(b_last - b_col + ig_col - m_new)
            c_st[h] = g_inter * c_prev + _dot_tn(kb, (g_s * v).astype(BF16))
            n_st[h:h + 1, :] = g_inter * n_prev + jnp.sum(g_s * k, axis=0, keepdims=True)
            m_st[h:h + 1, :] = jnp.broadcast_to(m_new, (1, LANES))

            hh = hh * lax.rsqrt(jnp.mean(hh * hh, axis=1, keepdims=True) + RMS_EPS)
            hh = hh * hng_ref[:, h * HD:(h + 1) * HD]
            mix_ref[rows, h * HD:(h + 1) * HD] = jax.nn.sigmoid(o_pre) * hh

        poolbuf[POOL_PAD:POOL_PAD + CHUNK, :] = z_ref[rows, 4 * MW:4 * MW + PW]
        if chained:
            pos = pos0 + step * x_ref.shape[0] + r0 + tok
        else:
            pos = pos0 + tok
        for g, w in enumerate(POOL_WINDOWS):
            cols = slice(g * PGD, (g + 1) * PGD)
            xg = poolbuf[POOL_PAD:POOL_PAD + CHUNK, cols]
            tot = xg
            for dlt in range(1, w):
                tot = tot + poolbuf[POOL_PAD - dlt:POOL_PAD - dlt + CHUNK, cols]
            cnt = jnp.minimum(w, pos + 1).astype(F32)
            y = _dot((tot / cnt - xg).astype(BF16), wpool_ref[g]) * pscale_ref[:, cols]
            mix_ref[rows, MW + g * PGD:MW + (g + 1) * PGD] = y
        poolbuf[0:POOL_PAD, :] = poolbuf[CHUNK:CHUNK + POOL_PAD, :]

        if not chained:
            c_out[c] = c_st[...]
            n_out[c] = n_st[...]
            m_out[c] = m_st[...]
            conv_out[c] = qkbuf[0:CONV_PAD, :]
            pool_out[c] = poolbuf[0:POOL_PAD, :]
        return carry

    lax.fori_loop(0, nb, chunk, 0)

    if chained:
        c_out[...] = c_st[...]
        n_out[...] = n_st[...]
        m_out[...] = m_st[...]
        conv_out[...] = qkbuf[0:CONV_PAD, :]
        pool_out[...] = poolbuf[0:POOL_PAD, :]

    hres = x + _dot(mix_ref[...].astype(BF16), wout_ref[...])
    hres_ref[...] = hres
    hn = _rmsnorm(hres, g2_ref[...])
    hn_ref[...] = hn
    hnb_ref[...] = hn.astype(BF16)


def _const_spec(shape):
    nd = len(shape)
    return pl.BlockSpec(shape, lambda i: (0,) * nd)


def _mixer(x, state, weights, *, pos0):
    t = x.shape[0]
    chained = state is None
    blk = MIX_BLOCK
    nb = blk // CHUNK
    ns = 1 if chained else t // CHUNK
    tok_spec = pl.BlockSpec((blk, D_MODEL), lambda i: (i, 0))
    w_specs = [_const_spec(w.shape) for w in weights]
    if chained:
        in_specs = [tok_spec] + w_specs
        args = (x,) + tuple(weights)
        st_specs = [_const_spec((NH, HD, HD)), _const_spec((SUBLANES, LANES)), _const_spec((SUBLANES, LANES)),
                    _const_spec((CONV_PAD, 2 * MW)), _const_spec((POOL_PAD, PW))]
        st_shapes = [(NH, HD, HD), (SUBLANES, LANES), (SUBLANES, LANES), (CONV_PAD, 2 * MW), (POOL_PAD, PW)]
    else:
        def sspec(*tail):
            nd = len(tail)
            return pl.BlockSpec((nb,) + tail, lambda i: (i,) + (0,) * nd)
        st_specs = [sspec(NH, HD, HD), sspec(SUBLANES, LANES), sspec(SUBLANES, LANES),
                    sspec(CONV_PAD, 2 * MW), sspec(POOL_PAD, PW)]
        st_shapes = [(ns, NH, HD, HD), (ns, SUBLANES, LANES), (ns, SUBLANES, LANES),
                     (ns, CONV_PAD, 2 * MW), (ns, POOL_PAD, PW)]
        in_specs = [tok_spec] + st_specs + w_specs
        args = (x,) + tuple(state) + tuple(weights)
    out_shape = ([jax.ShapeDtypeStruct((t, D_MODEL), F32)] * 2 + [jax.ShapeDtypeStruct((t, D_MODEL), BF16)]
                 + [jax.ShapeDtypeStruct(s, F32) for s in st_shapes])
    scratch = [
        pltpu.VMEM((blk, 4 * MW + PW), F32),
        pltpu.VMEM((blk, LANES), F32),
        pltpu.VMEM((blk, 2 * MW), F32),
        pltpu.VMEM((CONV_PAD + CHUNK, 2 * MW), F32),
        pltpu.VMEM((POOL_PAD + CHUNK, PW), F32),
        pltpu.VMEM((NH, HD, HD), F32),
        pltpu.VMEM((SUBLANES, LANES), F32),
        pltpu.VMEM((SUBLANES, LANES), F32),
    ]
    return pl.pallas_call(
        functools.partial(_mixer_body, chained, pos0),
        grid=(t // blk,),
        in_specs=in_specs,
        out_specs=[tok_spec, tok_spec, tok_spec] + st_specs,
        out_shape=out_shape,
        scratch_shapes=scratch,
        compiler_params=pltpu.CompilerParams(dimension_semantics=("arbitrary",),
                                             vmem_limit_bytes=VMEM_LIMIT_BYTES),
        name="mixer_chained" if chained else "mixer_streams",
    )(*args)


_CAND_GROUPS = (
    (("range", 0, 16), ("bcast", 0), None),
    (("range", 0, 8), ("bcast", 1), None),
    (("bcast", 0), ("range", 8, 16), None),
    (("bcast", 0), ("range", 0, 8), (2, 7)),
    (("bcast", 1), ("range", 0, 8), (2, 7)),
    (("bcast", 2), ("range", 0, 8), (2, 4)),
    (("bcast", 3), ("range", 0, 8), (2, 3)),
    (("bcast", 4), ("range", 0, 8), (2, 2)),
)


def _top16_by_rounds(s, key_id, break_ties):
    vals = []
    rank = jnp.full(s.shape, float(PEER_TOPK), F32)
    for a in range(PEER_TOPK):
        mx = jnp.max(s, axis=0, keepdims=True)
        hit = s == mx
        if break_ties:
            hit = key_id == jnp.min(jnp.where(hit, key_id, 1e9), axis=0, keepdims=True)
        s = jnp.where(hit, NEG_INF, s)
        rank = jnp.where(hit, float(a), rank)
        vals.append(mx)
    return vals, rank


def _taken(rank):
    return jnp.where(rank < float(PEER_TOPK), 1.0, 0.0)


def _route_body(hn_ref, wqh_ref, wql_ref, keys_ref, r2_ref, e2_ref, c1_ref, e1_ref, q_ref, s_ref):
    ntg = hn_ref.shape[0] // TOKEN_LANES
    hi, lo = _split_bf16(hn_ref[...])
    q_ref[...] = _dot(hi, wqh_ref[...]) + _dot(hi, wql_ref[...]) + _dot(lo, wqh_ref[...])
    for hp in range(2 * PEER_HEADS):
        k_hi, k_lo = _split_bf16(keys_ref[hp])
        q_hi, q_lo = _split_bf16(q_ref[:, hp * PEER_HALF:(hp + 1) * PEER_HALF])
        s_t = _dot_nt(k_hi, q_hi) + _dot_nt(k_hi, q_lo) + _dot_nt(k_lo, q_hi)
        for tg in range(ntg):
            s_ref[hp, tg] = s_t[:, tg * TOKEN_LANES:(tg + 1) * TOKEN_LANES]

    key_id = lax.broadcasted_iota(jnp.int32, (PEER_KEYS, TOKEN_LANES), 0).astype(F32)
    rank16 = lax.broadcasted_iota(jnp.int32, (PEER_TOPK, TOKEN_LANES), 0).astype(F32)
    pos_rows, valid_rows = [], []
    for ga, gb, vb in _CAND_GROUPS:
        n = ga[2] - ga[1] if ga[0] == "range" else gb[2] - gb[1]
        r = lax.broadcasted_iota(jnp.int32, (n, TOKEN_LANES), 0)
        a_idx = r + ga[1] if ga[0] == "range" else jnp.full_like(r, ga[1])
        b_idx = r + gb[1] if gb[0] == "range" else jnp.full_like(r, gb[1])
        pos_rows.append((a_idx * PEER_TOPK + b_idx).astype(F32))
        valid_rows.append(jnp.ones(r.shape, jnp.bool_) if vb is None else (b_idx >= vb[0]) & (b_idx <= vb[1]))
    cand_valid = jnp.concatenate(valid_rows, axis=0)
    cand_pos = jnp.where(cand_valid, jnp.concatenate(pos_rows, axis=0), -1.0)

    def side(arr, spec):
        if spec[0] == "range":
            return arr[spec[1]:spec[2], :]
        return arr[spec[1]:spec[1] + 1, :]

    def route(h, tg, break_ties):
        s1 = s_ref[2 * h, tg]
        s2 = s_ref[2 * h + 1, tg]
        v1, r1 = _top16_by_rounds(s1, key_id, break_ties)
        v2, r2 = _top16_by_rounds(s2, key_id, break_ties)
        v1a = jnp.concatenate(v1, axis=0)
        v2a = jnp.concatenate(v2, axis=0)
        cand = jnp.concatenate([side(v1a, ga) + side(v2a, gb) for ga, gb, _ in _CAND_GROUPS], axis=0)
        cand = jnp.where(cand_valid, cand, NEG_INF)
        top = v1[0] + v2[0]
        _, r3 = _top16_by_rounds(cand, cand_pos, break_ties)
        took = _taken(r3)
        z = jnp.sum(took * jnp.exp(cand - top), axis=0, keepdims=True)
        cnt = jnp.zeros((PEER_TOPK, TOKEN_LANES), F32)
        row0 = 0
        for ga, gb, _ in _CAND_GROUPS:
            n = ga[2] - ga[1] if ga[0] == "range" else gb[2] - gb[1]
            grp = took[row0:row0 + n, :]
            row0 += n
            if ga[0] == "range":
                if n < PEER_TOPK:
                    grp = jnp.concatenate([grp, jnp.zeros((PEER_TOPK - n, TOKEN_LANES), F32)], axis=0)
                cnt = cnt + grp
            else:
                cnt = cnt + jnp.where(rank16 == float(ga[1]), jnp.sum(grp, axis=0, keepdims=True), 0.0)
        r1b = r1.astype(BF16)
        c1 = jnp.zeros((PEER_KEYS, TOKEN_LANES), BF16)
        for a in range(PEER_TOPK):
            cnt_a = jnp.broadcast_to(cnt[a:a + 1, :], (PEER_KEYS, TOKEN_LANES)).astype(BF16)
            c1 = jnp.where(r1b == jnp.asarray(a, BF16), cnt_a, c1)
        r2_ref[h, tg] = _pack_rows(r2.astype(BF16))
        e2_ref[h, tg] = _pack_rows(jnp.exp(s2 - v2[0]).astype(BF16))
        c1_ref[h, tg] = c1.astype(F32)
        e1_ref[h, tg] = jnp.exp(s1 - v1[0]) * (1.0 / z)
        n_taken = (jnp.sum(_taken(r1), axis=0, keepdims=True), jnp.sum(_taken(r2), axis=0, keepdims=True),
                   jnp.sum(took, axis=0, keepdims=True))
        return n_taken

    def one(idx, carry):
        h = idx // ntg
        tg = idx % ntg
        n1, n2, n3 = route(h, tg, break_ties=False)
        extra = jnp.max(jnp.maximum(jnp.maximum(n1, n2), n3))

        @pl.when(extra > float(PEER_TOPK))
        def _():
            route(h, tg, break_ties=True)

        return carry

    lax.fori_loop(0, PEER_HEADS * ntg, one, 0)


def _peer_route(hn, wq_hi, wq_lo, keys):
    t = hn.shape[0]
    blk = ROUTE_BLOCK
    ntg = blk // TOKEN_LANES
    tab = (PEER_HEADS, t // TOKEN_LANES, PEER_KEYS, TOKEN_LANES)
    tab_spec = pl.BlockSpec((PEER_HEADS, ntg, PEER_KEYS, TOKEN_LANES), lambda i: (0, i, 0, 0))
    ptab = (PEER_HEADS, t // TOKEN_LANES, PEER_KEYS // 2, TOKEN_LANES)
    ptab_spec = pl.BlockSpec((PEER_HEADS, ntg, PEER_KEYS // 2, TOKEN_LANES), lambda i: (0, i, 0, 0))
    return pl.pallas_call(
        _route_body,
        grid=(t // blk,),
        in_specs=[pl.BlockSpec((blk, D_MODEL), lambda i: (i, 0)), _const_spec(wq_hi.shape),
                  _const_spec(wq_lo.shape), _const_spec(keys.shape)],
        out_specs=[ptab_spec, ptab_spec, tab_spec, tab_spec],
        out_shape=[jax.ShapeDtypeStruct(ptab, U32), jax.ShapeDtypeStruct(ptab, U32),
                   jax.ShapeDtypeStruct(tab, F32), jax.ShapeDtypeStruct(tab, F32)],
        scratch_shapes=[pltpu.VMEM((blk, 2 * PEER_HEADS * PEER_HALF), F32),
                        pltpu.VMEM((2 * PEER_HEADS, ntg, PEER_KEYS, TOKEN_LANES), F32)],
        compiler_params=pltpu.CompilerParams(dimension_semantics=("arbitrary",),
                                             vmem_limit_bytes=VMEM_LIMIT_BYTES),
        name="peer_route",
    )(hn, wq_hi, wq_lo, keys)


def _dense_body(final, nc, hn_ref, hres_ref, r2_ref, e2_ref, c1_ref, e1_ref, u_ref, vt_ref, fg_ref,
                y_ref, acc_ref, a0_ref, a1_ref, p0_ref, p1_ref):
    step = pl.program_id(0)
    ntg = hn_ref.shape[0] // TOKEN_LANES
    rows_per_step = u_ref.shape[0] // PEER_KEYS
    tile = (DENSE_TILE_ROWS, TOKEN_LANES)
    chunk_mid =(step + nc - 1) % nc
    chunk_out = (step + nc - 2) % nc

    @pl.when(step == 0)
    def _():
        for ref in (a0_ref, a1_ref, p0_ref, p1_ref):
            ref[...] = jnp.zeros(ref.shape, ref.dtype)

    @pl.when((chunk_out == 0) | (step == 0))
    def _():
        acc_ref[...] = jnp.zeros(acc_ref.shape, F32)

    def stages(a_in, a_mid, p_mid, p_out):
        def weigh(ii, jj, tg):
            i = chunk_mid * rows_per_step + ii
            lanes = slice(tg * TOKEN_LANES, (tg + 1) * TOKEN_LANES)
            j0 = jj * DENSE_TILE_ROWS
            a = a_mid[ii * PEER_KEYS + j0:ii * PEER_KEYS + j0 + DENSE_TILE_ROWS, lanes]
            act = a * (0.5 + 0.5 * lax.erf(a * (0.5 ** 0.5)))
            w = jnp.zeros(tile, BF16)
            packed = slice(j0 // 2, (j0 + DENSE_TILE_ROWS) // 2)
            for h in range(PEER_HEADS):
                cnt_i = jnp.broadcast_to(c1_ref[h, tg, pl.ds(i, 1), :], tile).astype(BF16)
                e1_i = jnp.broadcast_to(e1_ref[h, tg, pl.ds(i, 1), :], tile).astype(BF16)
                sel = _unpack_rows(r2_ref[h, tg, packed, :]) < cnt_i
                w = w + jnp.where(sel, _unpack_rows(e2_ref[h, tg, packed, :]), jnp.zeros((), BF16)) * e1_i
            p0 = (ii * PEER_KEYS + j0) // 2
            p_mid[p0:p0 + DENSE_TILE_ROWS // 2, lanes] = _pack_rows(w * act.astype(BF16))

        tiles = [(ii, jj, tg) for ii in range(rows_per_step) for jj in range(PEER_KEYS // DENSE_TILE_ROWS)
                 for tg in range(ntg)]
        pieces = []
        n_exp, n_tok = a_in.shape
        for m0 in range(0, n_exp, DENSE_M_PIECE):
            for t0 in range(0, n_tok, DENSE_N_PIECE):
                pieces.append(("in", m0, t0))
        for m0 in range(0, D_MODEL, DENSE_M_PIECE):
            for t0 in range(0, n_tok, DENSE_N_PIECE):
                pieces.append(("out", m0, t0))
        per_piece = -(-len(tiles) // len(pieces))
        for k, (kind, m0, t0) in enumerate(pieces):
            ms, ts = slice(m0, m0 + DENSE_M_PIECE), slice(t0, t0 + DENSE_N_PIECE)
            if kind == "in":
                a_in[ms, ts] = _dot_nt(u_ref[ms, :], hn_ref[ts, :])
            else:
                acc_ref[ms, ts] += _dot(vt_ref[ms, :], _unpack_rows(p_out[:, ts]))
            for ii, jj, tg in tiles[k * per_piece:(k + 1) * per_piece]:
                weigh(ii, jj, tg)

    @pl.when(step % 2 == 0)
    def _():
        stages(a0_ref, a1_ref, p1_ref, p0_ref)

    @pl.when(step % 2 == 1)
    def _():
        stages(a1_ref, a0_ref, p0_ref, p1_ref)

    @pl.when((chunk_out == nc - 1) & (step >= 2))
    def _():
        y = hres_ref[...] + acc_ref[...].T
        if final:
            y = _rmsnorm(y, fg_ref[...])
        y_ref[...] = y


def _peer_dense(hn, hres, tables, u_bf, vt_bf, fg, *, final):
    t = hn.shape[0]
    blk = DENSE_BLOCK
    ntg = blk // TOKEN_LANES
    nblk = t // blk
    nc = N_EXPERTS // EXPERT_CHUNK
    pairs = nblk * nc

    def blk_of(g):
        return jnp.clip(g, 0, pairs - 1) // nc

    def chunk_of(g):
        return jnp.clip(g, 0, pairs - 1) % nc

    def tok_spec(lag):
        return pl.BlockSpec((blk, D_MODEL), lambda g: (blk_of(g - lag), 0))

    tab_spec = pl.BlockSpec((PEER_HEADS, ntg, PEER_KEYS, TOKEN_LANES), lambda g: (0, blk_of(g - 1), 0, 0))
    ptab_spec = pl.BlockSpec((PEER_HEADS, ntg, PEER_KEYS // 2, TOKEN_LANES), lambda g: (0, blk_of(g - 1), 0, 0))
    return pl.pallas_call(
        functools.partial(_dense_body, final, nc),
        grid=(pairs + 2,),
        in_specs=[tok_spec(0), tok_spec(2), ptab_spec, ptab_spec, tab_spec, tab_spec,
                  pl.BlockSpec((EXPERT_CHUNK, D_MODEL), lambda g: (chunk_of(g), 0)),
                  pl.BlockSpec((D_MODEL, EXPERT_CHUNK), lambda g: (0, chunk_of(g - 2))),
                  pl.BlockSpec((1, D_MODEL), lambda g: (0, 0))],
        out_specs=tok_spec(2),
        out_shape=jax.ShapeDtypeStruct((t, D_MODEL), F32),
        scratch_shapes=[pltpu.VMEM((D_MODEL, blk), F32),
                        pltpu.VMEM((EXPERT_CHUNK, blk), F32), pltpu.VMEM((EXPERT_CHUNK, blk), F32),
                        pltpu.VMEM((EXPERT_CHUNK // 2, blk), U32), pltpu.VMEM((EXPERT_CHUNK // 2, blk), U32)],
        compiler_params=pltpu.CompilerParams(dimension_semantics=("arbitrary",),
                                             vmem_limit_bytes=VMEM_LIMIT_BYTES),
        name="peer_dense",
    )(hn, hres, *tables, u_bf, vt_bf, fg)


PAST_LEN = 1024


def _row(a):
    return a.reshape(1, -1).astype(F32)


def _layer(x, state, pos0, mixer_w, peer_w, fg, final):
    hres, hn, hn_bf, *new_state = _mixer(x, state, mixer_w, pos0=pos0)
    wq_hi, wq_lo, keys, u_bf, vt_bf = peer_w
    tables = _peer_route(hn, wq_hi, wq_lo, keys)
    y = _peer_dense(hn_bf, hres, tables, u_bf, vt_bf, fg, final=final)
    return y, new_state


def kernel(x_prompt, x_sample, state_mlstm_C, state_mlstm_n, state_mlstm_m, state_conv, state_pool, norm1_g, w_in, b_gate, conv_w, conv_b, head_norm_g, w_pool, pool_scale, w_out, norm2_g, w_query, sub_keys, expert_u, expert_v, final_norm_g):
    depth = w_in.shape[0]
    bp, seq, d = x_prompt.shape
    bs, dseq, _ = x_sample.shape
    assert bp == 1 and dseq == CHUNK and d == D_MODEL
    yp = x_prompt.reshape(seq, d)
    ys = x_sample.reshape(bs * dseq, d)
    fg = _row(final_norm_g)
    new_p, new_s = [], []
    for l in range(depth):
        gate0 = 4 * MW
        gate1 = gate0 + 2 * NH
        w_main = jnp.concatenate([w_in[l][:, :gate0], w_in[l][:, gate1:]], axis=1).astype(BF16)
        w_gate = jnp.pad(w_in[l][:, gate0:gate1], ((0, 0), (0, LANES - 2 * NH))).astype(BF16)
        bg = jnp.pad(b_gate[l].astype(F32), (0, LANES - 2 * NH)).reshape(1, LANES)
        mixer_w = (_row(norm1_g[l]), w_main, w_gate, bg, conv_w[l].astype(F32), _row(conv_b[l]),
                   _row(head_norm_g[l]), w_pool[l].astype(BF16), _row(pool_scale[l]),
                   w_out[l].astype(BF16), _row(norm2_g[l]))
        wq_hi, wq_lo = _split_bf16(w_query[l].astype(F32))
        keys = sub_keys[l].astype(F32).reshape(2 * PEER_HEADS, PEER_KEYS, PEER_HALF)
        peer_w = (wq_hi, wq_lo, keys, expert_u[l].astype(BF16), expert_v[l].T.astype(BF16))
        state = (
            state_mlstm_C[l].astype(F32),
            jnp.pad(state_mlstm_n[l].astype(F32), ((0, 0), (0, SUBLANES - NH), (0, 0))),
            jnp.broadcast_to(jnp.pad(state_mlstm_m[l].astype(F32), ((0, 0), (0, SUBLANES - NH)))[:, :, None],
                             (bs, SUBLANES, LANES)),
            jnp.pad(state_conv[l].astype(F32), ((0, 0), (CONV_PAD - (CONV_WIDTH - 1), 0), (0, 0))),
            jnp.pad(state_pool[l].astype(F32), ((0, 0), (1, 0), (0, 0))),
        )
        final = l == depth - 1
        yp, sp = _layer(yp, None, 0, mixer_w, peer_w, fg, final)
        ys, ss = _layer(ys, state, PAST_LEN, mixer_w, peer_w, fg, final)
        c, n, m, cv, po = sp
        new_p.append((c[None], n[None, :NH], m[None, :NH, 0], cv[None, CONV_PAD - (CONV_WIDTH - 1):], po[None, 1:]))
        c, n, m, cv, po = ss
        new_s.append((c, n[:, :NH], m[:, :NH, 0], cv[:, CONV_PAD - (CONV_WIDTH - 1):], po[:, 1:]))

    def stk(lst, i):
        return jnp.stack([e[i] for e in lst])

    return (yp.reshape(bp, seq, d), ys.reshape(bs, dseq, d),
            stk(new_p, 0), stk(new_p, 1), stk(new_p, 2), stk(new_p, 3), stk(new_p, 4),
            stk(new_s, 0), stk(new_s, 1), stk(new_s, 2), stk(new_s, 3), stk(new_s, 4))
```
